```python
import math
import jax, jax.numpy as jnp
from jax import lax
import numpy as np

D_MODEL = 1024
BATCH = 32
SEQ = 256
DEPTH = 4
DEC_BATCH = 4
DEC_SEQ = 2048
PAST_LEN = 256

GRID_W = 64
N_MIXERS = 2
N_SSM_LAYERS = (DEPTH + 1) // 2
N_POOL_LAYERS = DEPTH // 2
SSM_GROUP = 16
SSM_GROUPS = D_MODEL // SSM_GROUP
SSM_STATE = 64
N_DIRS = 2
POOL_WINDOWS = (2, 4, 8, 16)
N_POOL_GROUPS = len(POOL_WINDOWS)
POOL_GROUP = D_MODEL // N_POOL_GROUPS
D_FF = -(-8 * D_MODEL // (3 * 256)) * 256
DEEPNORM_ALPHA = (2 * DEPTH) ** 0.25
DEEPNORM_BETA = (8 * DEPTH) ** -0.25
LN_EPS = 1e-5
DT_MIN = 0.001
DT_MAX = 0.1

kernel_name = 'hybrid_s5_pool_prefix_diffusion_step'

F32 = jnp.float32


def layer_norm(x, g, b):
    xf = x.astype(F32)
    mu = jnp.mean(xf, axis=-1, keepdims=True)
    var = jnp.mean(jnp.square(xf - mu), axis=-1, keepdims=True)
    y = (xf - mu) * lax.rsqrt(var + LN_EPS)
    return (y * g + b).astype(x.dtype)


def ada_mod(cond, w, b):
    m = (jax.nn.silu(cond) @ w + b)[..., None, :]
    return jnp.split(m, 6, axis=-1)


def modulate(x, shift, scale):
    return x * (1 + scale) + shift


def s5_discretize(a_re, a_im, log_step, b_re, b_im):
    a_re = a_re.astype(F32)
    a_im = a_im.astype(F32)
    b_re = b_re.astype(F32)
    b_im = b_im.astype(F32)
    dt = jnp.exp(log_step.astype(F32))[:, None]
    mag = jnp.exp(a_re * dt)
    ang = a_im * dt
    ab_re = mag * jnp.cos(ang)
    ab_im = mag * jnp.sin(ang)
    den = jnp.square(a_re) + jnp.square(a_im)
    num_re = ab_re - 1.0
    coef_re = (num_re * a_re + ab_im * a_im) / den
    coef_im = (ab_im * a_re - num_re * a_im) / den
    bb_re = coef_re[..., None] * b_re - coef_im[..., None] * b_im
    bb_im = coef_re[..., None] * b_im + coef_im[..., None] * b_re
    return ab_re, ab_im, bb_re, bb_im


def _linear_recurrence_combine(e1, e2):
    a1r, a1i, b1r, b1i = e1
    a2r, a2i, b2r, b2i = e2
    return (a2r * a1r - a2i * a1i,
            a2r * a1i + a2i * a1r,
            a2r * b1r - a2i * b1i + b2r,
            a2r * b1i + a2i * b1r + b2i)


def s5_scan(u, ab_re, ab_im, bb_re, bb_im, h0_re, h0_im, reverse):
    bu_re = jnp.einsum('blgh,gph->blgp', u, bb_re)
    bu_im = jnp.einsum('blgh,gph->blgp', u, bb_im)
    first = -1 if reverse else 0
    h0_re = h0_re.astype(F32)
    h0_im = h0_im.astype(F32)
    bu_re = bu_re.at[:, first].add(ab_re * h0_re - ab_im * h0_im)
    bu_im = bu_im.at[:, first].add(ab_re * h0_im + ab_im * h0_re)
    a_re = jnp.broadcast_to(ab_re, bu_re.shape)
    a_im = jnp.broadcast_to(ab_im, bu_re.shape)
    _, _, s_re, s_im = lax.associative_scan(
        _linear_recurrence_combine, (a_re, a_im, bu_re, bu_im), axis=1, reverse=reverse)
    last = 0 if reverse else -1
    return s_re, s_im, s_re[:, last], s_im[:, last]


def s5_mixer(h, a_re, a_im, log_step, b_re, b_im, c_re, c_im, d_skip, glu_w, glu_b, h0_re, h0_im):
    bsz, seq_len, _ = h.shape
    u = h.astype(F32).reshape(bsz, seq_len, SSM_GROUPS, SSM_GROUP)
    y = u * d_skip.astype(F32).reshape(SSM_GROUPS, SSM_GROUP)
    fin_re, fin_im = [], []
    for d in range(N_DIRS):
        ab_re, ab_im, bb_re, bb_im = s5_discretize(a_re[d], a_im[d], log_step[d], b_re[d], b_im[d])
        s_re, s_im, f_re, f_im = s5_scan(u, ab_re, ab_im, bb_re, bb_im,
                                         h0_re[:, d], h0_im[:, d], reverse=(d == 1))
        y = y + (jnp.einsum('blgp,ghp->blgh', s_re, c_re[d].astype(F32))
                 - jnp.einsum('blgp,ghp->blgh', s_im, c_im[d].astype(F32)))
        fin_re.append(f_re)
        fin_im.append(f_im)
    y = jax.nn.gelu(y.reshape(bsz, seq_len, D_MODEL))
    v = y @ glu_w + glu_b
    val, gate = jnp.split(v, 2, axis=-1)
    out = val * jax.nn.sigmoid(gate)
    return out.astype(h.dtype), jnp.stack(fin_re, axis=1), jnp.stack(fin_im, axis=1)


def _window_bounds(n, k):
    pos = jnp.arange(n)
    lo = jnp.clip(pos - k // 2, 0, n)
    hi = jnp.clip(pos - k // 2 + k, 0, n)
    return lo, hi


def pool_seq(x, k):
    seq_len = x.shape[1]
    s = jnp.pad(jnp.cumsum(x, axis=1), ((0, 0), (1, 0), (0, 0)))
    lo, hi = _window_bounds(seq_len, k)
    cnt = (hi - lo).astype(F32)[None, :, None]
    return (jnp.take(s, hi, axis=1) - jnp.take(s, lo, axis=1)) / cnt


def pool_grid(x, k):
    bsz, seq_len, ch = x.shape
    rows = seq_len // GRID_W
    g = x.reshape(bsz, rows, GRID_W, ch)
    s = jnp.cumsum(jnp.cumsum(g, axis=1), axis=2)
    s = jnp.pad(s, ((0, 0), (1, 0), (1, 0), (0, 0)))
    rlo, rhi = _window_bounds(rows, k)
    clo, chi = _window_bounds(GRID_W, k)
    s_rhi = jnp.take(s, rhi, axis=1)
    s_rlo = jnp.take(s, rlo, axis=1)
    tot = (jnp.take(s_rhi, chi, axis=2) - jnp.take(s_rhi, clo, axis=2)
           - jnp.take(s_rlo, chi, axis=2) + jnp.take(s_rlo, clo, axis=2))
    cnt = ((rhi - rlo)[:, None] * (chi - clo)[None, :]).astype(F32)[None, :, :, None]
    return (tot / cnt).reshape(bsz, seq_len, ch)


def pool_mixer(h, w_groups, scale, grid):
    hf = h.astype(F32)
    outs = []
    for gi, k in enumerate(POOL_WINDOWS):
        xg = hf[..., gi * POOL_GROUP:(gi + 1) * POOL_GROUP]
        pooled = pool_grid(xg, k) if grid else pool_seq(xg, k)
        outs.append((pooled - xg) @ w_groups[gi])
    return (jnp.concatenate(outs, axis=-1) * scale).astype(h.dtype)


def swiglu(h, w_gate, w_up, w_down):
    return (jax.nn.silu(h @ w_gate) * (h @ w_up)) @ w_down


def setup_inputs(seed: int = 0) -> dict:
    key = jax.random.key(seed)
    ks = jax.random.split(key, 32)
    nrm = jax.random.normal
    G, P, HG = SSM_GROUPS, SSM_STATE, SSM_GROUP
    x_prompt = nrm(ks[0], (BATCH, SEQ, D_MODEL), F32)
    x_sample = nrm(ks[1], (DEC_BATCH, DEC_SEQ, D_MODEL), F32)
    state_ssm_re = 0.1 * nrm(ks[2], (DEC_BATCH, N_SSM_LAYERS, N_DIRS, G, P), F32)
    state_ssm_im = 0.1 * nrm(ks[3], (DEC_BATCH, N_SSM_LAYERS, N_DIRS, G, P), F32)
    c = nrm(ks[4], (DEC_BATCH, D_MODEL), F32)
    c_ctx = nrm(ks[5], (D_MODEL,), F32)
    ada_w = nrm(ks[6], (DEPTH, D_MODEL, 6 * D_MODEL), F32) * D_MODEL ** -0.5
    ada_b = 0.01 * nrm(ks[7], (DEPTH, 6 * D_MODEL), F32)
    ln_g = 1.0 + 0.01 * nrm(ks[8], (DEPTH, 2, D_MODEL), F32)
    ln_b = 0.01 * nrm(ks[9], (DEPTH, 2, D_MODEL), F32)
    ssm_a_re = -0.5 + 0.01 * nrm(ks[10], (N_SSM_LAYERS, N_DIRS, G, P), F32)
    ssm_a_im = (math.pi * jnp.arange(P, dtype=F32)
                + 0.01 * nrm(ks[11], (N_SSM_LAYERS, N_DIRS, G, P), F32))
    ssm_log_step = jax.random.uniform(ks[12], (N_SSM_LAYERS, N_DIRS, G), F32,
                                      minval=math.log(DT_MIN), maxval=math.log(DT_MAX))
    ssm_b_re = nrm(ks[13], (N_SSM_LAYERS, N_DIRS, G, P, HG), F32) * (2 * HG) ** -0.5
    ssm_b_im = nrm(ks[14], (N_SSM_LAYERS, N_DIRS, G, P, HG), F32) * (2 * HG) ** -0.5
    ssm_c_re = nrm(ks[15], (N_SSM_LAYERS, N_DIRS, G, HG, P), F32) * (2 * P) ** -0.5
    ssm_c_im = nrm(ks[16], (N_SSM_LAYERS, N_DIRS, G, HG, P), F32) * (2 * P) ** -0.5
    ssm_d = nrm(ks[17], (N_SSM_LAYERS, D_MODEL), F32)
    ssm_glu_w = nrm(ks[18], (N_SSM_LAYERS, D_MODEL, 2 * D_MODEL), F32) * (D_MODEL ** -0.5 * DEEPNORM_BETA)
    ssm_glu_b = 0.01 * nrm(ks[19], (N_SSM_LAYERS, 2 * D_MODEL), F32)
    pool_w = nrm(ks[20], (N_POOL_LAYERS, N_POOL_GROUPS, POOL_GROUP, POOL_GROUP), F32) * (POOL_GROUP ** -0.5 * DEEPNORM_BETA)
    pool_scale = 1.0 + 0.01 * nrm(ks[21], (N_POOL_LAYERS, D_MODEL), F32)
    ffn_w_gate = nrm(ks[22], (DEPTH, D_MODEL, D_FF), F32) * D_MODEL ** -0.5
    ffn_w_up = nrm(ks[23], (DEPTH, D_MODEL, D_FF), F32) * D_MODEL ** -0.5
    ffn_w_down = nrm(ks[24], (DEPTH, D_FF, D_MODEL), F32) * (D_FF ** -0.5 * DEEPNORM_BETA)
    return {'x_prompt': x_prompt, 'x_sample': x_sample,
            'state_ssm_re': state_ssm_re, 'state_ssm_im': state_ssm_im,
            'c': c, 'c_ctx': c_ctx, 'ada_w': ada_w, 'ada_b': ada_b,
            'ln_g': ln_g, 'ln_b': ln_b,
            'ssm_a_re': ssm_a_re, 'ssm_a_im': ssm_a_im, 'ssm_log_step': ssm_log_step,
            'ssm_b_re': ssm_b_re, 'ssm_b_im': ssm_b_im, 'ssm_c_re': ssm_c_re, 'ssm_c_im': ssm_c_im,
            'ssm_d': ssm_d, 'ssm_glu_w': ssm_glu_w, 'ssm_glu_b': ssm_glu_b,
            'pool_w': pool_w, 'pool_scale': pool_scale,
            'ffn_w_gate': ffn_w_gate, 'ffn_w_up': ffn_w_up, 'ffn_w_down': ffn_w_down}


def reference(x_prompt, x_sample, state_ssm_re, state_ssm_im, c, c_ctx, ada_w, ada_b,
              ln_g, ln_b, ssm_a_re, ssm_a_im, ssm_log_step, ssm_b_re, ssm_b_im,
              ssm_c_re, ssm_c_im, ssm_d, ssm_glu_w, ssm_glu_b, pool_w, pool_scale,
              ffn_w_gate, ffn_w_up, ffn_w_down):
    xc = x_prompt
    xl = x_sample
    new_re, new_im = [], []
    for i in range(DEPTH):
        sh1c, sc1c, g1c, sh2c, sc2c, g2c = ada_mod(c_ctx, ada_w[i], ada_b[i])
        sh1l, sc1l, g1l, sh2l, sc2l, g2l = ada_mod(c, ada_w[i], ada_b[i])
        hc = modulate(xc, sh1c, sc1c)
        hl = modulate(xl, sh1l, sc1l)
        j = i // N_MIXERS
        if i % N_MIXERS == 0:
            params = (ssm_a_re[j], ssm_a_im[j], ssm_log_step[j], ssm_b_re[j], ssm_b_im[j],
                      ssm_c_re[j], ssm_c_im[j], ssm_d[j], ssm_glu_w[j], ssm_glu_b[j])
            zeros = jnp.zeros((xc.shape[0], N_DIRS, SSM_GROUPS, SSM_STATE), F32)
            oc, fin_re, fin_im = s5_mixer(hc, *params, zeros, zeros)
            ol, _, _ = s5_mixer(hl, *params, state_ssm_re[:, j], state_ssm_im[:, j])
            new_re.append(fin_re)
            new_im.append(fin_im)
        else:
            oc = pool_mixer(hc, pool_w[j], pool_scale[j], grid=False)
            ol = pool_mixer(hl, pool_w[j], pool_scale[j], grid=True)
        xc = layer_norm(DEEPNORM_ALPHA * xc + g1c * oc, ln_g[i, 0], ln_b[i, 0])
        xl = layer_norm(DEEPNORM_ALPHA * xl + g1l * ol, ln_g[i, 0], ln_b[i, 0])
        fc = swiglu(modulate(xc, sh2c, sc2c), ffn_w_gate[i], ffn_w_up[i], ffn_w_down[i])
        fl = swiglu(modulate(xl, sh2l, sc2l), ffn_w_gate[i], ffn_w_up[i], ffn_w_down[i])
        xc = layer_norm(DEEPNORM_ALPHA * xc + g2c * fc, ln_g[i, 1], ln_b[i, 1])
        xl = layer_norm(DEEPNORM_ALPHA * xl + g2l * fl, ln_g[i, 1], ln_b[i, 1])
    new_state_ssm_re = jnp.stack(new_re, axis=1).astype(state_ssm_re.dtype)
    new_state_ssm_im = jnp.stack(new_im, axis=1).astype(state_ssm_im.dtype)
    return (xc, xl, new_state_ssm_re, new_state_ssm_im)
```

```python
import functools
import math

import jax
import jax.numpy as jnp
from jax import lax
from jax.experimental import pallas as pl
from jax.experimental.pallas import tpu as pltpu

F32 = jnp.float32
BF16 = jnp.bfloat16

D_MODEL = 1024
DEPTH = 4
SSM_GROUP = 16
SSM_GROUPS = D_MODEL // SSM_GROUP
SSM_STATE = 64
GRID_W = 64
POOL_WINDOWS = (2, 4, 8, 16)
POOL_GROUP = D_MODEL // len(POOL_WINDOWS)
D_FF = 2816
DEEPNORM_ALPHA = (2 * DEPTH) ** 0.25
LN_EPS = 1e-5

CHUNK = 16
LANES = 128
N_COND = 8
CTX_ROW = 4
VMEM_LIMIT = 56 * 1024 * 1024


def _cparams(sem):
    return pltpu.CompilerParams(dimension_semantics=sem, vmem_limit_bytes=VMEM_LIMIT)


def _layer_norm(z, g, b):
    mu = jnp.mean(z, axis=-1, keepdims=True)
    zc = z - mu
    var = jnp.mean(zc * zc, axis=-1, keepdims=True)
    return zc * lax.rsqrt(var + LN_EPS) * g + b


def _ada_kernel(c_ref, w_ref, b_ref, o_ref):
    c = c_ref[...]
    s = (c * jax.nn.sigmoid(c)).astype(BF16)
    o_ref[0] = jnp.dot(s, w_ref[0].astype(BF16), preferred_element_type=F32) + b_ref[0]


def ada_call(cond, ada_w, ada_b):
    tn = 1536
    n6 = 6 * D_MODEL
    return pl.pallas_call(
        _ada_kernel,
        grid=(DEPTH, n6 // tn),
        in_specs=[pl.BlockSpec((N_COND, D_MODEL), lambda i, n: (0, 0)),
                  pl.BlockSpec((1, D_MODEL, tn), lambda i, n: (i, 0, n)),
                  pl.BlockSpec((1, 1, tn), lambda i, n: (i, 0, n))],
        out_specs=pl.BlockSpec((1, N_COND, tn), lambda i, n: (i, 0, n)),
        out_shape=jax.ShapeDtypeStruct((DEPTH, N_COND, n6), F32),
        compiler_params=_cparams(("parallel", "parallel")),
        name="ada",
    )(cond, ada_w, ada_b)


def _cmul(ar, ai, br, bi):
    return ar * br - ai * bi, ar * bi + ai * br


def _s5_prep_kernel(a_re_ref, a_im_ref, ls_ref, b1_ref, b2_ref, b3_ref, b4_ref, cx_ref, cy_ref,
                    mb_ref, mf_ref, sc_ref, *, n_pow):
    d = (pl.program_id(0) // SSM_GROUPS) % 2
    a_re = a_re_ref[0]
    a_im = a_im_ref[0]
    dt = jnp.exp(ls_ref[0])
    lam = a_re * dt
    ang = a_im * dt
    mag = jnp.exp(lam)
    ab_re = mag * jnp.cos(ang)
    ab_im = mag * jnp.sin(ang)
    den = a_re * a_re + a_im * a_im
    num_re = ab_re - 1.0
    coef_re = (num_re * a_re + ab_im * a_im) / den
    coef_im = (ab_im * a_re - num_re * a_im) / den

    m_idx = lax.broadcasted_iota(jnp.int32, (32, LANES), 0)
    p_re = jnp.ones((32, LANES), F32)
    p_im = jnp.zeros((32, LANES), F32)
    q_re, q_im = ab_re, ab_im
    for bit in range(5):
        n_re, n_im = _cmul(p_re, p_im, q_re, q_im)
        take = ((m_idx >> bit) & 1) == 1
        p_re = jnp.where(take, n_re, p_re)
        p_im = jnp.where(take, n_im, p_im)
        q_re, q_im = _cmul(q_re, q_im, q_re, q_im)
    lane = lax.broadcasted_iota(jnp.int32, (1, LANES), 1)
    first_half = lane < SSM_STATE
    p_im_signed = jnp.where(first_half, -p_im, p_im)

    s_re, s_im = p_re[CHUNK:CHUNK + 1], p_im[CHUNK:CHUNK + 1]
    row_idx = lax.broadcasted_iota(jnp.int32, (LANES, LANES), 0)
    tab = jnp.zeros((LANES, LANES), F32)
    for j in range(n_pow):
        tab = jnp.where(row_idx == j, jnp.where(first_half, s_re, s_im), tab)
        s_re, s_im = _cmul(s_re, s_im, s_re, s_im)
    sc_ref[0] = tab.T[:, :8]

    ba = coef_re * b1_ref[0] + coef_im * b2_ref[0]
    bb = coef_re * b3_ref[0] + coef_im * b4_ref[0]
    ba_t = jnp.concatenate([ba] * CHUNK, axis=0)
    bb_t = jnp.concatenate([bb] * CHUNK, axis=0)
    cx_t = jnp.concatenate([cx_ref[0]] * CHUNK, axis=0)
    cy_t = jnp.concatenate([cy_ref[0]] * CHUNK, axis=0)

    def expand(tab, ks):
        return jnp.concatenate(
            [jnp.broadcast_to(tab[k:k + 1], (SSM_GROUP, LANES)) for k in ks], axis=0)

    lane256 = lax.broadcasted_iota(jnp.int32, (SSM_GROUP, 2 * LANES), 1)

    def build(k_in, k_out, shifts, masks):
        mbt = expand(p_re, k_in) * ba_t + expand(p_im_signed, k_in) * bb_t
        mb = mbt.T
        mb_ref[0] = mb.astype(BF16)
        g = jnp.dot(cx_ref[0], mb, precision=lax.Precision.HIGHEST, preferred_element_type=F32)
        for t in range(CHUNK):
            blk = pltpu.roll(g, shifts[t], axis=1) if shifts[t] else g
            mf_ref[0, CHUNK * t:CHUNK * (t + 1), 0:2 * LANES] = jnp.where(
                masks(t, lane256), blk, 0.0).astype(BF16)
        mc = cx_t * expand(p_re, k_out) + cy_t * expand(p_im, k_out)
        mf_ref[0, :, 2 * LANES:3 * LANES] = mc.astype(BF16)

    @pl.when(d == 0)
    def _():
        build([CHUNK - 1 - s for s in range(CHUNK)], [t + 1 for t in range(CHUNK)],
              [(2 * LANES - SSM_GROUP * (CHUNK - 1 - t)) % (2 * LANES) for t in range(CHUNK)],
              lambda t, l: l < SSM_GROUP * (t + 1))

    @pl.when(d == 1)
    def _():
        build(list(range(CHUNK)), [CHUNK - t for t in range(CHUNK)],
              [SSM_GROUP * t for t in range(CHUNK)],
              lambda t, l: l >= SSM_GROUP * t)


def s5_prep_call(a_re2, a_im2, log_step, b1, b2, b3, b4, cx, cy):
    n = a_re2.shape[0]
    n_pow = 7
    row = lambda shape: pl.BlockSpec((1,) + shape, lambda i: (i, 0, 0))
    return pl.pallas_call(
        functools.partial(_s5_prep_kernel, n_pow=n_pow),
        grid=(n,),
        in_specs=[row((1, LANES)), row((1, LANES)), row((1, 1)),
                  row((SSM_GROUP, LANES)), row((SSM_GROUP, LANES)),
                  row((SSM_GROUP, LANES)), row((SSM_GROUP, LANES)),
                  row((SSM_GROUP, LANES)), row((SSM_GROUP, LANES))],
        out_specs=[row((LANES, 2 * LANES)), row((2 * LANES, 3 * LANES)), row((LANES, 8))],
        out_shape=[jax.ShapeDtypeStruct((n, LANES, 2 * LANES), BF16),
                   jax.ShapeDtypeStruct((n, 2 * LANES, 3 * LANES), BF16),
                   jax.ShapeDtypeStruct((n, LANES, 8), F32)],
        compiler_params=_cparams(("parallel",)),
        name="s5_prep",
    )(a_re2, a_im2, log_step, b1, b2, b3, b4, cx, cy)


TOK_TILE = CHUNK * LANES


def _relayout_in_kernel(x_ref, mod_ref, o_ref):
    m = mod_ref[0]
    shift, scale = m[0:1], 1.0 + m[1:2]
    for s in range(CHUNK):
        v = x_ref[pl.ds(s, LANES, stride=CHUNK), :]
        o_ref[s] = (v * scale + shift).T


def relayout_in(x, mod_i, row_of_tile):
    t = x.shape[0]
    return pl.pallas_call(
        _relayout_in_kernel,
        grid=(t // TOK_TILE, D_MODEL // LANES),
        in_specs=[pl.BlockSpec((TOK_TILE, LANES), lambda i, j: (i, j)),
                  pl.BlockSpec((1, 6, LANES), lambda i, j: (row_of_tile(i), 0, j))],
        out_specs=pl.BlockSpec((CHUNK, LANES, LANES), lambda i, j: (0, j, i)),
        out_shape=jax.ShapeDtypeStruct((CHUNK, D_MODEL, t // CHUNK), F32),
        compiler_params=_cparams(("parallel", "parallel")),
        name="relayout_in",
    )(x, mod_i)


def _gelu_tanh(x):
    return 0.5 * x * (1.0 + jnp.tanh(math.sqrt(2.0 / math.pi) * (x + 0.044715 * (x * x * x))))


def _relayout_out_kernel(y_ref, o_ref):
    for s in range(CHUNK):
        o_ref[pl.ds(s, LANES, stride=CHUNK), :] = _gelu_tanh(y_ref[s]).T


def relayout_out(yt):
    t = yt.shape[2] * CHUNK
    return pl.pallas_call(
        _relayout_out_kernel,
        grid=(t // TOK_TILE, D_MODEL // LANES),
        in_specs=[pl.BlockSpec((CHUNK, LANES, LANES), lambda i, j: (0, j, i))],
        out_specs=pl.BlockSpec((TOK_TILE, LANES), lambda i, j: (i, j)),
        out_shape=jax.ShapeDtypeStruct((t, D_MODEL), F32),
        compiler_params=_cparams(("parallel", "parallel")),
        name="relayout_out",
    )(yt)


CONV_GROUPS = 4


def _chunk_scan(s_re, s_im, sc, n_seq_chunks, reverse, h0):
    lane = lax.broadcasted_iota(jnp.int32, (SSM_STATE, LANES), 1)
    cpos = lane & (n_seq_chunks - 1)
    first = (n_seq_chunks - 1) if reverse else 0
    x_re, x_im = s_re, s_im
    if h0 is not None:
        a_re, a_im = sc[0:SSM_STATE, 0:1], sc[SSM_STATE:, 0:1]
        t_re, t_im = _cmul(a_re, a_im, h0[0], h0[1])
        x_re = x_re + jnp.where(cpos == first, t_re, 0.0)
        x_im = x_im + jnp.where(cpos == first, t_im, 0.0)
    n_steps = n_seq_chunks.bit_length() - 1
    for j in range(n_steps):
        sh = 1 << j
        a_re, a_im = sc[0:SSM_STATE, j:j + 1], sc[SSM_STATE:, j:j + 1]
        if reverse:
            ok = cpos < n_seq_chunks - sh
            r_re = pltpu.roll(x_re, LANES - sh, axis=1)
            r_im = pltpu.roll(x_im, LANES - sh, axis=1)
        else:
            ok = cpos >= sh
            r_re = pltpu.roll(x_re, sh, axis=1)
            r_im = pltpu.roll(x_im, sh, axis=1)
        r_re = jnp.where(ok, r_re, 0.0)
        r_im = jnp.where(ok, r_im, 0.0)
        t_re, t_im = _cmul(a_re, a_im, r_re, r_im)
        x_re = x_re + t_re
        x_im = x_im + t_im
    if reverse:
        ok = cpos < n_seq_chunks - 1
        i_re = pltpu.roll(x_re, LANES - 1, axis=1)
        i_im = pltpu.roll(x_im, LANES - 1, axis=1)
    else:
        ok = cpos >= 1
        i_re = pltpu.roll(x_re, 1, axis=1)
        i_im = pltpu.roll(x_im, 1, axis=1)
    z_re = h0[0] if h0 is not None else 0.0
    z_im = h0[1] if h0 is not None else 0.0
    i_re = jnp.where(ok, i_re, z_re)
    i_im = jnp.where(ok, i_im, z_im)
    return (i_re, i_im), (x_re, x_im)


def _s5_conv_kernel(*refs, n_seq_chunks, has_h0, want_final):
    if has_h0:
        z_ref, mb_ref, mf_ref, sc_ref, ds_ref, h0_ref = refs[:6]
        rest = refs[6:]
    else:
        z_ref, mb_ref, mf_ref, sc_ref, ds_ref = refs[:5]
        h0_ref = None
        rest = refs[5:]
    y_ref = rest[0]
    fin_ref = rest[1] if want_final else None
    nc = z_ref.shape[2]
    n_blk = nc // LANES
    seq_per_blk = LANES // n_seq_chunks
    n_seq = n_blk * seq_per_blk

    for gl in range(CONV_GROUPS):
        r = z_ref[:, SSM_GROUP * gl:SSM_GROUP * (gl + 1), :].reshape(CHUNK * SSM_GROUP, nc)
        rb = r.astype(BF16)
        y = ds_ref[gl] * r
        for d in range(2):
            sc = sc_ref[d, gl]
            s_all = jnp.dot(mb_ref[d, gl], rb, preferred_element_type=F32)
            hin_re, hin_im, hout_re, hout_im = [], [], [], []
            for blk in range(n_blk):
                cols = slice(blk * LANES, (blk + 1) * LANES)
                h0 = None
                if has_h0:
                    lane = lax.broadcasted_iota(jnp.int32, (SSM_STATE, LANES), 1)
                    h0_re = jnp.zeros((SSM_STATE, LANES), F32)
                    h0_im = jnp.zeros((SSM_STATE, LANES), F32)
                    for q in range(seq_per_blk):
                        b = blk * seq_per_blk + q
                        in_seq = (lane // n_seq_chunks) == q
                        h0_re = jnp.where(in_seq, h0_ref[d, gl, 0:SSM_STATE, b:b + 1], h0_re)
                        h0_im = jnp.where(in_seq, h0_ref[d, gl, SSM_STATE:, b:b + 1], h0_im)
                    h0 = (h0_re, h0_im)
                (i_re, i_im), (o_re, o_im) = _chunk_scan(
                    s_all[0:SSM_STATE, cols], s_all[SSM_STATE:, cols], sc, n_seq_chunks, d == 1, h0)
                hin_re.append(i_re)
                hin_im.append(i_im)
                hout_re.append(o_re)
                hout_im.append(o_im)
            cat = lambda xs: xs[0] if len(xs) == 1 else jnp.concatenate(xs, axis=1)
            rhs = jnp.concatenate([rb, cat(hin_re).astype(BF16), cat(hin_im).astype(BF16)], axis=0)
            y = y + jnp.dot(mf_ref[d, gl], rhs, preferred_element_type=F32)
            if want_final:
                last = 0 if d == 1 else n_seq_chunks - 1
                col = lax.broadcasted_iota(jnp.int32, (n_seq, nc), 1)
                seq = lax.broadcasted_iota(jnp.int32, (n_seq, nc), 0)
                sel = (col == seq * n_seq_chunks + last).astype(F32)
                hout = jnp.concatenate([cat(hout_re), cat(hout_im)], axis=0)
                fin_ref[d, gl] = lax.dot_general(
                    sel, hout, (((1,), (1,)), ((), ())),
                    precision=lax.Precision.HIGHEST, preferred_element_type=F32)
        y_ref[:, SSM_GROUP * gl:SSM_GROUP * (gl + 1), :] = y.reshape(CHUNK, SSM_GROUP, nc)


def s5_conv_call(zt, mb, mf, sc, ds, h0, n_seq_chunks, want_final):
    nc = zt.shape[2]
    n_seq = nc // n_seq_chunks
    gb = CONV_GROUPS
    in_specs = [pl.BlockSpec((CHUNK, SSM_GROUP * gb, nc), lambda g: (0, g, 0)),
                pl.BlockSpec((2, gb, LANES, 2 * LANES), lambda g: (0, g, 0, 0)),
                pl.BlockSpec((2, gb, 2 * LANES, 3 * LANES), lambda g: (0, g, 0, 0)),
                pl.BlockSpec((2, gb, LANES, 8), lambda g: (0, g, 0, 0)),
                pl.BlockSpec((gb, CHUNK * SSM_GROUP, 1), lambda g: (g, 0, 0))]
    args = [zt, mb, mf, sc, ds]
    if h0 is not None:
        in_specs.append(pl.BlockSpec((2, gb, LANES, n_seq), lambda g: (0, g, 0, 0)))
        args.append(h0)
    out_specs = [pl.BlockSpec((CHUNK, SSM_GROUP * gb, nc), lambda g: (0, g, 0))]
    out_shape = [jax.ShapeDtypeStruct(zt.shape, F32)]
    if want_final:
        out_specs.append(pl.BlockSpec((2, gb, n_seq, LANES), lambda g: (0, g, 0, 0)))
        out_shape.append(jax.ShapeDtypeStruct((2, SSM_GROUPS, n_seq, LANES), F32))
    return pl.pallas_call(
        functools.partial(_s5_conv_kernel, n_seq_chunks=n_seq_chunks, has_h0=h0 is not None,
                          want_final=want_final),
        grid=(SSM_GROUPS // gb,),
        in_specs=in_specs, out_specs=out_specs, out_shape=out_shape,
        compiler_params=_cparams(("parallel",)),
        name="s5_conv",
    )(*args)


def _glu_ln_kernel(x_ref, y_ref, mod_ref, w_ref, b_ref, lng_ref, lnb_ref, o_ref):
    m = mod_ref[0]
    v = jnp.dot(y_ref[...].astype(BF16), w_ref[...], preferred_element_type=F32) + b_ref[...]
    out = v[:, :D_MODEL] * jax.nn.sigmoid(v[:, D_MODEL:])
    z = DEEPNORM_ALPHA * x_ref[...] + m[2:3] * out
    o_ref[...] = _layer_norm(z, lng_ref[...], lnb_ref[...])


def glu_ln_call(x, yg, mod_i, row_of_tile, w_bf16, b, lng, lnb, tm=512):
    t = x.shape[0]
    const = lambda shape: pl.BlockSpec(shape, lambda i: (0,) * len(shape))
    return pl.pallas_call(
        _glu_ln_kernel,
        grid=(t // tm,),
        in_specs=[pl.BlockSpec((tm, D_MODEL), lambda i: (i, 0)),
                  pl.BlockSpec((tm, D_MODEL), lambda i: (i, 0)),
                  pl.BlockSpec((1, 6, D_MODEL), lambda i: (row_of_tile(i, tm), 0, 0)),
                  const((D_MODEL, 2 * D_MODEL)), const((1, 2 * D_MODEL)),
                  const((1, D_MODEL)), const((1, D_MODEL))],
        out_specs=pl.BlockSpec((tm, D_MODEL), lambda i: (i, 0)),
        out_shape=jax.ShapeDtypeStruct((t, D_MODEL), F32),
        compiler_params=_cparams(("parallel",)),
        name="glu_ln",
    )(x, yg, mod_i, w_bf16, b, lng, lnb)


def _ffn_ln_kernel(x_ref, mod_ref, wg_ref, wu_ref, wd_ref, lng_ref, lnb_ref, o_ref):
    m = mod_ref[0]
    x = x_ref[...]
    h = (x * (1.0 + m[4:5]) + m[3:4]).astype(BF16)
    g = jnp.dot(h, wg_ref[...], preferred_element_type=F32)
    u = jnp.dot(h, wu_ref[...], preferred_element_type=F32)
    a = (g * jax.nn.sigmoid(g) * u).astype(BF16)
    f = jnp.dot(a, wd_ref[...], preferred_element_type=F32)
    z = DEEPNORM_ALPHA * x + m[5:6] * f
    o_ref[...] = _layer_norm(z, lng_ref[...], lnb_ref[...])


def ffn_ln_call(x, mod_i, row_of_tile, wg, wu, wd, lng, lnb, tm=256):
    t = x.shape[0]
    const = lambda shape: pl.BlockSpec(shape, lambda i: (0,) * len(shape))
    return pl.pallas_call(
        _ffn_ln_kernel,
        grid=(t // tm,),
        in_specs=[pl.BlockSpec((tm, D_MODEL), lambda i: (i, 0)),
                  pl.BlockSpec((1, 6, D_MODEL), lambda i: (row_of_tile(i, tm), 0, 0)),
                  const((D_MODEL, D_FF)), const((D_MODEL, D_FF)), const((D_FF, D_MODEL)),
                  const((1, D_MODEL)), const((1, D_MODEL))],
        out_specs=pl.BlockSpec((tm, D_MODEL), lambda i: (i, 0)),
        out_shape=jax.ShapeDtypeStruct((t, D_MODEL), F32),
        compiler_params=_cparams(("parallel",)),
        name="ffn_ln",
    )(x, mod_i, wg, wu, wd, lng, lnb)


POOL_TILE = 256
GRID_PAD = max(POOL_WINDOWS) // 2


def _window_matrix(k, period):
    t = lax.broadcasted_iota(jnp.int32, (POOL_TILE, POOL_TILE), 0)
    s = lax.broadcasted_iota(jnp.int32, (POOL_TILE, POOL_TILE), 1)
    pos = t & (period - 1)
    base = t - pos
    lo = base + jnp.clip(pos - k // 2, 0, period)
    hi = base + jnp.clip(pos - k // 2 + k, 0, period)
    return jnp.where((s >= lo) & (s < hi), 1.0, 0.0).astype(BF16)


def _window_count(k, period, pos):
    pos = pos & (period - 1)
    lo = jnp.clip(pos - k // 2, 0, period)
    hi = jnp.clip(pos - k // 2 + k, 0, period)
    return (hi - lo).astype(F32)


def _pool_ln_kernel(x_ref, mod_ref, w_ref, sc_ref, lng_ref, lnb_ref, o_ref, *scratch, grid):
    m = mod_ref[0]
    n_tok = x_ref.shape[0]
    n_tiles = n_tok // POOL_TILE
    period = GRID_W if grid else POOL_TILE
    if grid:
        (pad_ref,) = scratch
        n_rows = n_tok // GRID_W
        zeros = jnp.zeros((GRID_PAD * GRID_W, POOL_GROUP), F32)
    for gi, k in enumerate(POOL_WINDOWS):
        ch = slice(gi * POOL_GROUP, (gi + 1) * POOL_GROUP)
        shift, scale = m[0:1, ch], 1.0 + m[1:2, ch]
        win = _window_matrix(k, period)
        cnt = _window_count(k, period, lax.broadcasted_iota(jnp.int32, (POOL_TILE, POOL_GROUP), 0))
        if grid:
            pad_ref[0:GRID_PAD * GRID_W, :] = zeros
            pad_ref[(GRID_PAD + n_rows) * GRID_W:, :] = zeros
        for tile in range(n_tiles):
            rows = slice(tile * POOL_TILE, (tile + 1) * POOL_TILE)
            xg = x_ref[rows, ch] * scale + shift
            hi = xg.astype(BF16)
            lo = (xg - hi.astype(F32)).astype(BF16)
            tot = (jnp.dot(win, hi, preferred_element_type=F32)
                   + jnp.dot(win, lo, preferred_element_type=F32))
            if grid:
                pad_ref[GRID_PAD * GRID_W + tile * POOL_TILE:
                        GRID_PAD * GRID_W + (tile + 1) * POOL_TILE, :] = tot
            else:
                diff = tot / cnt - xg
                o_ref[rows, ch] = jnp.dot(diff.astype(BF16), w_ref[gi], preferred_element_type=F32)
        if grid:
            col_cnt = _window_count(k, period, lax.broadcasted_iota(jnp.int32, (GRID_W, POOL_GROUP), 0))

            def row_body(r, carry, k=k, ch=ch, col_cnt=col_cnt):
                start = r + (GRID_PAD - k // 2)
                acc = pad_ref[pl.ds(pl.multiple_of(start * GRID_W, GRID_W), GRID_W), :]
                for jj in range(1, k):
                    acc = acc + pad_ref[pl.ds(pl.multiple_of((start + jj) * GRID_W, GRID_W), GRID_W), :]
                row_cnt = (jnp.minimum(r - k // 2 + k, n_rows) - jnp.maximum(r - k // 2, 0)).astype(F32)
                rows = pl.ds(pl.multiple_of(r * GRID_W, GRID_W), GRID_W)
                o_ref[rows, ch] = acc / (col_cnt * row_cnt) - (x_ref[rows, ch] * scale + shift)
                return carry

            lax.fori_loop(0, n_rows, row_body, 0)
            for tile in range(n_tiles):
                rows = slice(tile * POOL_TILE, (tile + 1) * POOL_TILE)
                o_ref[rows, ch] = jnp.dot(o_ref[rows, ch].astype(BF16), w_ref[gi],
                                          preferred_element_type=F32)
    z = DEEPNORM_ALPHA * x_ref[...] + m[2:3] * (o_ref[...] * sc_ref[...])
    o_ref[...] = _layer_norm(z, lng_ref[...], lnb_ref[...])


def pool_ln_call(x, mod_i, row_of_tile, w_bf16, scale, lng, lnb, grid, tm):
    t = x.shape[0]
    const = lambda shape: pl.BlockSpec(shape, lambda i: (0,) * len(shape))
    scratch = []
    if grid:
        scratch = [pltpu.VMEM((tm + 2 * GRID_PAD * GRID_W, POOL_GROUP), F32)]
    return pl.pallas_call(
        functools.partial(_pool_ln_kernel, grid=grid),
        grid=(t // tm,),
        in_specs=[pl.BlockSpec((tm, D_MODEL), lambda i: (i, 0)),
                  pl.BlockSpec((1, 6, D_MODEL), lambda i: (row_of_tile(i, tm), 0, 0)),
                  const((len(POOL_WINDOWS), POOL_GROUP, POOL_GROUP)), const((1, D_MODEL)),
                  const((1, D_MODEL)), const((1, D_MODEL))],
        out_specs=pl.BlockSpec((tm, D_MODEL), lambda i: (i, 0)),
        out_shape=jax.ShapeDtypeStruct((t, D_MODEL), F32),
        scratch_shapes=scratch,
        compiler_params=_cparams(("parallel",)),
        name="pool_grid_ln" if grid else "pool_seq_ln",
    )(x, mod_i, w_bf16, scale, lng, lnb)


def _dup(v):
    return jnp.concatenate([v, v], axis=-1)


def _s5_operators(ssm_a_re, ssm_a_im, ssm_log_step, ssm_b_re, ssm_b_im, ssm_c_re, ssm_c_im):
    n_layers = ssm_a_re.shape[0]
    n = n_layers * 2 * SSM_GROUPS
    a_re2 = _dup(ssm_a_re).reshape(n, 1, LANES)
    a_im2 = _dup(ssm_a_im).reshape(n, 1, LANES)
    ls = ssm_log_step.reshape(n, 1, 1)
    bt_re = jnp.swapaxes(ssm_b_re, -1, -2).reshape(n, SSM_GROUP, SSM_STATE)
    bt_im = jnp.swapaxes(ssm_b_im, -1, -2).reshape(n, SSM_GROUP, SSM_STATE)
    cat = lambda u, v: jnp.concatenate([u, v], axis=-1)
    b1, b2 = cat(bt_re, bt_im), cat(-bt_im, bt_re)
    b3, b4 = cat(bt_im, bt_re), cat(bt_re, -bt_im)
    c_re = ssm_c_re.reshape(n, SSM_GROUP, SSM_STATE)
    c_im = ssm_c_im.reshape(n, SSM_GROUP, SSM_STATE)
    cx, cy = cat(c_re, -c_im), cat(-c_im, -c_re)
    mb, mf, sc = s5_prep_call(a_re2, a_im2, ls, b1, b2, b3, b4, cx, cy)
    shp = (n_layers, 2, SSM_GROUPS)
    return (mb.reshape(shp + mb.shape[1:]), mf.reshape(shp + mf.shape[1:]), sc.reshape(shp + sc.shape[1:]))


def kernel(x_prompt, x_sample, state_ssm_re, state_ssm_im, c, c_ctx, ada_w, ada_b, ln_g, ln_b, ssm_a_re, ssm_a_im, ssm_log_step, ssm_b_re, ssm_b_im, ssm_c_re, ssm_c_im, ssm_d, ssm_glu_w, ssm_glu_b, pool_w, pool_scale, ffn_w_gate, ffn_w_up, ffn_w_down):
    batch, seq, _ = x_prompt.shape
    dec_batch, dec_seq, _ = x_sample.shape
    assert dec_batch <= CTX_ROW and seq == POOL_TILE and dec_seq % TOK_TILE == 0
    assert (batch * seq) % TOK_TILE == 0

    cond = jnp.zeros((N_COND, D_MODEL), F32).at[:dec_batch].set(c).at[CTX_ROW].set(c_ctx)
    mod = ada_call(cond, ada_w, ada_b.reshape(DEPTH, 1, 6 * D_MODEL))
    mod = mod.reshape(DEPTH, N_COND, 6, D_MODEL)

    mb, mf, sc = _s5_operators(ssm_a_re, ssm_a_im, ssm_log_step, ssm_b_re, ssm_b_im, ssm_c_re, ssm_c_im)
    n_ssm = ssm_a_re.shape[0]
    ds = jnp.tile(ssm_d.reshape(n_ssm, SSM_GROUPS, 1, SSM_GROUP), (1, 1, CHUNK, 1))
    ds = ds.reshape(n_ssm, SSM_GROUPS, CHUNK * SSM_GROUP, 1)
    h0 = jnp.concatenate([state_ssm_re, state_ssm_im], axis=-1)
    h0 = jnp.transpose(h0, (1, 2, 3, 4, 0))

    xc = x_prompt.reshape(batch * seq, D_MODEL)
    xl = x_sample.reshape(dec_batch * dec_seq, D_MODEL)
    ctx_row = lambda i, tm=TOK_TILE: CTX_ROW
    lat_row = lambda i, tm=TOK_TILE: (i * tm) // dec_seq

    glu_w = ssm_glu_w.astype(BF16)
    pw = pool_w.astype(BF16)
    wg, wu, wd = ffn_w_gate.astype(BF16), ffn_w_up.astype(BF16), ffn_w_down.astype(BF16)
    row1 = lambda v: v.reshape(1, -1)

    new_re, new_im = [], []
    for i in range(DEPTH):
        j = i // 2
        mod_i = mod[i]
        lng0, lnb0 = row1(ln_g[i, 0]), row1(ln_b[i, 0])
        lng1, lnb1 = row1(ln_g[i, 1]), row1(ln_b[i, 1])
        if i % 2 == 0:
            glu_b = row1(ssm_glu_b[j])
            zt = relayout_in(xc, mod_i, ctx_row)
            yt, fin = s5_conv_call(zt, mb[j], mf[j], sc[j], ds[j], None, seq // CHUNK, True)
            xc = glu_ln_call(xc, relayout_out(yt), mod_i, ctx_row, glu_w[j], glu_b, lng0, lnb0)
            fin = jnp.transpose(fin, (2, 0, 1, 3))
            new_re.append(fin[..., :SSM_STATE])
            new_im.append(fin[..., SSM_STATE:])
            zt = relayout_in(xl, mod_i, lat_row)
            (yt,) = s5_conv_call(zt, mb[j], mf[j], sc[j], ds[j], h0[j], dec_seq // CHUNK, False)
            xl = glu_ln_call(xl, relayout_out(yt), mod_i, lat_row, glu_w[j], glu_b, lng0, lnb0)
        else:
            xc = pool_ln_call(xc, mod_i, ctx_row, pw[j], row1(pool_scale[j]), lng0, lnb0, False, seq)
            xl = pool_ln_call(xl, mod_i, lat_row, pw[j], row1(pool_scale[j]), lng0, lnb0, True, dec_seq)
        xc = ffn_ln_call(xc, mod_i, ctx_row, wg[i], wu[i], wd[i], lng1, lnb1)
        xl = ffn_ln_call(xl, mod_i, lat_row, wg[i], wu[i], wd[i], lng1, lnb1)

    new_state_re = jnp.stack(new_re, axis=1).astype(state_ssm_re.dtype)
    new_state_im = jnp.stack(new_im, axis=1).astype(state_ssm_im.dtype)
    return (xc.reshape(batch, seq, D_MODEL), xl.reshape(dec_batch, dec_seq, D_MODEL),
            new_state_re, new_state_im)
```
